```python
import jax, jax.numpy as jnp
from jax import lax
import numpy as np

D_MODEL = 2048
BATCH = 2
SEQ = 8192
DEPTH = 1

CHUNK = 64
D_MIX = D_MODEL
D_MLSTM = D_MIX // 2
MLSTM_HEADS = 4
MLSTM_HEAD_DIM = D_MLSTM // MLSTM_HEADS
D_RGLRU = D_MIX - D_MLSTM
RG_BLOCKS = 8
RG_BLOCK_DIM = D_RGLRU // RG_BLOCKS
CONV_WIDTH = 4
RG_C = 8.0
D_FF = 5632
FFN_WEIGHT = 0.5
ALPHA = float((2 * DEPTH) ** 0.25)
BETA = float((8 * DEPTH) ** -0.25)
LN_EPS = 1e-5
N_SUBLAYERS = 3
IN_COLS = 4 * D_MLSTM + 2 * MLSTM_HEADS + 2 * D_RGLRU

kernel_name = "hybrid_mlstm_rglru_macaron_block"


def _layer_norm(x, gain=None, bias=None):
    xf = x.astype(jnp.float32)
    mu = xf.mean(-1, keepdims=True)
    var = jnp.square(xf - mu).mean(-1, keepdims=True)
    y = (xf - mu) * lax.rsqrt(var + LN_EPS)
    if gain is not None:
        y = y * gain.astype(jnp.float32) + bias.astype(jnp.float32)
    return y.astype(x.dtype)


def _swiglu(u, w13, w2):
    g, v = jnp.split(u @ w13, 2, axis=-1)
    return (jax.nn.silu(g) * v) @ w2


def _mlstm(q, k, v, i_pre, f_pre):
    B, S, H, Dh = q.shape
    nc = S // CHUNK

    def to_chunks(t):
        t = t.astype(jnp.float32).reshape((B, nc, CHUNK, H) + t.shape[3:])
        return jnp.moveaxis(jnp.moveaxis(t, 1, 0), 3, 2)

    qc = to_chunks(q)
    kc = to_chunks(k) * (Dh ** -0.5)
    vc = to_chunks(v)
    li = to_chunks(i_pre)
    lf = to_chunks(jax.nn.log_sigmoid(f_pre.astype(jnp.float32)))
    tri = jnp.tril(jnp.ones((CHUNK, CHUNK), dtype=bool))

    def step(carry, xs):
        C, n, m = carry
        qb, kb, vb, lib, lfb = xs
        b = jnp.cumsum(lfb, axis=-1)
        d = b[..., :, None] - b[..., None, :] + lib[..., None, :]
        d = jnp.where(tri, d, -jnp.inf)
        inter = b + m[..., None]
        m_t = jnp.maximum(inter, d.max(-1))
        w_intra = jnp.exp(d - m_t[..., None])
        w_inter = jnp.exp(inter - m_t)
        s = jnp.einsum('bhtd,bhsd->bhts', qb, kb) * w_intra
        num = (w_inter[..., None] * jnp.einsum('bhtd,bhde->bhte', qb, C)
               + jnp.einsum('bhts,bhse->bhte', s, vb))
        den = w_inter * jnp.einsum('bhtd,bhd->bht', qb, n) + s.sum(-1)
        h = num / jnp.maximum(jnp.abs(den), jnp.exp(-m_t))[..., None]
        b_last = b[..., -1]
        w_log = b_last[..., None] - b + lib
        m_new = jnp.maximum(b_last + m, w_log.max(-1))
        decay = jnp.exp(b_last + m - m_new)
        w_s = jnp.exp(w_log - m_new[..., None])
        C = decay[..., None, None] * C + jnp.einsum('bhs,bhsd,bhse->bhde', w_s, kb, vb)
        n = decay[..., None] * n + jnp.einsum('bhs,bhsd->bhd', w_s, kb)
        return (C, n, m_new), h

    init = (jnp.zeros((B, H, Dh, Dh), jnp.float32),
            jnp.zeros((B, H, Dh), jnp.float32),
            jnp.zeros((B, H), jnp.float32))
    _, hs = lax.scan(step, init, (qc, kc, vc, li, lf))
    hs = jnp.transpose(hs, (1, 0, 3, 2, 4)).reshape(B, S, H, Dh)
    return hs.astype(v.dtype)


def _causal_depthwise_conv(x, w, b):
    S = x.shape[1]
    xp = jnp.pad(x, ((0, 0), (CONV_WIDTH - 1, 0), (0, 0)))
    y = b
    for tap in range(CONV_WIDTH):
        y = y + xp[:, tap:tap + S] * w[tap]
    return y


def _rg_lru(x, wa, ba, wx, bx, lam):
    B, S, C = x.shape
    xb = x.reshape(B, S, RG_BLOCKS, RG_BLOCK_DIM)
    r = jax.nn.sigmoid(jnp.einsum('bsnd,nde->bsne', xb, wa).reshape(B, S, C) + ba)
    i = jax.nn.sigmoid(jnp.einsum('bsnd,nde->bsne', xb, wx).reshape(B, S, C) + bx)
    log_a = -RG_C * r.astype(jnp.float32) * jax.nn.softplus(-lam.astype(jnp.float32))
    a = jnp.exp(log_a)
    u = jnp.sqrt(-jnp.expm1(2.0 * log_a)) * (i * x).astype(jnp.float32)

    def combine(left, right):
        a1, b1 = left
        a2, b2 = right
        return a1 * a2, a2 * b1 + b2

    _, h = lax.associative_scan(combine, (a, u), axis=1)
    return h.astype(x.dtype)


def setup_inputs(seed: int = 0) -> dict:
    key = jax.random.key(seed)
    ks = jax.random.split(key, 24)
    nrm = jax.random.normal
    f32 = jnp.float32
    x = nrm(ks[0], (BATCH, SEQ, D_MODEL), f32)
    c = nrm(ks[1], (BATCH, D_MODEL), f32)
    w_ada = nrm(ks[2], (DEPTH, D_MODEL, N_SUBLAYERS * 3 * D_MODEL), f32) * D_MODEL ** -0.5
    b_ada = 0.01 * nrm(ks[3], (DEPTH, N_SUBLAYERS * 3 * D_MODEL), f32)
    ffn1_w13 = nrm(ks[4], (DEPTH, D_MODEL, 2 * D_FF), f32) * D_MODEL ** -0.5
    ffn1_w2 = nrm(ks[5], (DEPTH, D_FF, D_MODEL), f32) * (D_FF ** -0.5 * BETA)
    w_in = nrm(ks[6], (DEPTH, D_MODEL, IN_COLS), f32) * D_MODEL ** -0.5
    f_start = 4 * D_MLSTM + MLSTM_HEADS
    b_in = 0.01 * nrm(ks[7], (DEPTH, IN_COLS), f32)
    b_in = b_in.at[:, f_start:f_start + MLSTM_HEADS].add(
        jnp.linspace(3.0, 6.0, MLSTM_HEADS, dtype=f32))
    mlstm_norm_g = 1.0 + 0.02 * nrm(ks[8], (DEPTH, MLSTM_HEADS, MLSTM_HEAD_DIM), f32)
    rg_conv_w = nrm(ks[9], (DEPTH, CONV_WIDTH, D_RGLRU), f32) * CONV_WIDTH ** -0.5
    rg_conv_b = 0.01 * nrm(ks[10], (DEPTH, D_RGLRU), f32)
    rg_wa = nrm(ks[11], (DEPTH, RG_BLOCKS, RG_BLOCK_DIM, RG_BLOCK_DIM), f32) * RG_BLOCK_DIM ** -0.5
    rg_ba = 0.01 * nrm(ks[12], (DEPTH, D_RGLRU), f32)
    rg_wx = nrm(ks[13], (DEPTH, RG_BLOCKS, RG_BLOCK_DIM, RG_BLOCK_DIM), f32) * RG_BLOCK_DIM ** -0.5
    rg_bx = 0.01 * nrm(ks[14], (DEPTH, D_RGLRU), f32)
    a_pow = jax.random.uniform(ks[15], (DEPTH, D_RGLRU), f32, 0.9, 0.999)
    p = a_pow ** (1.0 / RG_C)
    rg_lambda = jnp.log(p) - jnp.log1p(-p)
    w_out = nrm(ks[16], (DEPTH, D_MIX, D_MODEL), f32) * (D_MIX ** -0.5 * BETA)
    ffn2_w13 = nrm(ks[17], (DEPTH, D_MODEL, 2 * D_FF), f32) * D_MODEL ** -0.5
    ffn2_w2 = nrm(ks[18], (DEPTH, D_FF, D_MODEL), f32) * (D_FF ** -0.5 * BETA)
    ln_g = 1.0 + 0.02 * nrm(ks[19], (DEPTH, N_SUBLAYERS, D_MODEL), f32)
    ln_b = 0.01 * nrm(ks[20], (DEPTH, N_SUBLAYERS, D_MODEL), f32)
    return {"x": x, "c": c, "w_ada": w_ada, "b_ada": b_ada,
            "ffn1_w13": ffn1_w13, "ffn1_w2": ffn1_w2,
            "w_in": w_in, "b_in": b_in, "mlstm_norm_g": mlstm_norm_g,
            "rg_conv_w": rg_conv_w, "rg_conv_b": rg_conv_b,
            "rg_wa": rg_wa, "rg_ba": rg_ba, "rg_wx": rg_wx, "rg_bx": rg_bx,
            "rg_lambda": rg_lambda, "w_out": w_out,
            "ffn2_w13": ffn2_w13, "ffn2_w2": ffn2_w2,
            "ln_g": ln_g, "ln_b": ln_b}


def reference(x, c, w_ada, b_ada, ffn1_w13, ffn1_w2, w_in, b_in, mlstm_norm_g,
              rg_conv_w, rg_conv_b, rg_wa, rg_ba, rg_wx, rg_bx, rg_lambda, w_out,
              ffn2_w13, ffn2_w2, ln_g, ln_b):
    B, S, _ = x.shape
    H, Dh = MLSTM_HEADS, MLSTM_HEAD_DIM
    for l in range(DEPTH):
        mod = (jax.nn.silu(c) @ w_ada[l] + b_ada[l]).reshape(B, N_SUBLAYERS, 3, D_MODEL)
        shift, scale, gate = mod[:, :, 0], mod[:, :, 1], mod[:, :, 2]

        def modulate(h, s):
            return _layer_norm(h) * (1.0 + scale[:, s, None, :]) + shift[:, s, None, :]

        y = _swiglu(modulate(x, 0), ffn1_w13[l], ffn1_w2[l])
        x = _layer_norm(ALPHA * x + FFN_WEIGHT * gate[:, 0, None, :] * y, ln_g[l, 0], ln_b[l, 0])

        u = modulate(x, 1)
        proj = u @ w_in[l] + b_in[l]
        q, k, v, o, i_pre, f_pre, x_rg, g_rg = jnp.split(
            proj, [D_MLSTM, 2 * D_MLSTM, 3 * D_MLSTM, 4 * D_MLSTM,
                   4 * D_MLSTM + H, 4 * D_MLSTM + 2 * H,
                   4 * D_MLSTM + 2 * H + D_RGLRU], axis=-1)
        h_m = _mlstm(q.reshape(B, S, H, Dh), k.reshape(B, S, H, Dh),
                     v.reshape(B, S, H, Dh), i_pre, f_pre)
        h_m = (_layer_norm(h_m) * mlstm_norm_g[l]).reshape(B, S, D_MLSTM) * jax.nn.sigmoid(o)
        x_rg = _causal_depthwise_conv(x_rg, rg_conv_w[l], rg_conv_b[l])
        h_r = _rg_lru(x_rg, rg_wa[l], rg_ba[l], rg_wx[l], rg_bx[l], rg_lambda[l])
        h_r = h_r * jax.nn.gelu(g_rg)
        y = jnp.concatenate([h_m, h_r], axis=-1) @ w_out[l]
        x = _layer_norm(ALPHA * x + gate[:, 1, None, :] * y, ln_g[l, 1], ln_b[l, 1])

        y = _swiglu(modulate(x, 2), ffn2_w13[l], ffn2_w2[l])
        x = _layer_norm(ALPHA * x + FFN_WEIGHT * gate[:, 2, None, :] * y, ln_g[l, 2], ln_b[l, 2])
    return x
```

```python
import functools

import jax
import jax.numpy as jnp
from jax import lax
from jax.experimental import pallas as pl
from jax.experimental.pallas import tpu as pltpu

F32 = jnp.float32
BF16 = jnp.bfloat16

N_SUB = 3
MLSTM_HEADS = 4
RG_BLOCK_DIM = 128
CONV_WIDTH = 4
RG_C = 8.0
FFN_WEIGHT = 0.5
DEPTH = 1
ALPHA = float((2 * DEPTH) ** 0.25)
LN_EPS = 1e-5

LANES = 128
SUBLANES = 8
VMEM_LIMIT = 56 * 1024 * 1024

ADA_TN = 1024
ADA_RC = 256
FFN_TM = 512
FFN_TF = 512
PROJ_TM = 1024
PROJ_TN = 1024
MLSTM_L = 256
RG_T = 1024
RG_CB = 512
OUT_TM = 512


def _ln(x):
    mu = jnp.mean(x, axis=-1, keepdims=True)
    xc = x - mu
    var = jnp.mean(xc * xc, axis=-1, keepdims=True)
    return xc * lax.rsqrt(var + LN_EPS)


def _sigmoid(x):
    return 1.0 / (1.0 + jnp.exp(-x))


def _log_sigmoid(x):
    return jnp.minimum(x, 0.0) - jnp.log1p(jnp.exp(-jnp.abs(x)))


def _softplus(x):
    return jnp.maximum(x, 0.0) + jnp.log1p(jnp.exp(-jnp.abs(x)))


def _gelu_tanh(x):
    c = 0.7978845608028654
    return 0.5 * x * (1.0 + jnp.tanh(c * (x + 0.044715 * (x * x * x))))


def _dot(a, b):
    return jnp.dot(a, b, preferred_element_type=F32)


def _adaln_kernel(ct_ref, w_ref, b_ref, o_ref, s_ref, *, nb):
    ct = ct_ref[...]
    s_ref[...] = ct * _sigmoid(ct)
    d = w_ref.shape[0]
    tn = w_ref.shape[1]

    def body(i, accs):
        r0 = pl.multiple_of(i * ADA_RC, ADA_RC)
        wblk = w_ref[pl.ds(r0, ADA_RC), :]
        sblk = s_ref[pl.ds(r0, ADA_RC), :]
        out = []
        for b in range(nb):
            p = wblk * sblk[:, b:b + 1]
            out.append(accs[b] + jnp.sum(p.reshape(ADA_RC // SUBLANES, SUBLANES, tn), axis=0))
        return tuple(out)

    accs = lax.fori_loop(0, d // ADA_RC, body,
                         tuple(jnp.zeros((SUBLANES, tn), F32) for _ in range(nb)))
    for b in range(nb):
        o_ref[b:b + 1, :] = jnp.sum(accs[b], axis=0, keepdims=True) + b_ref[...]


def _adaln(c, w, bias):
    nb, d = c.shape
    n = w.shape[1]
    return pl.pallas_call(
        functools.partial(_adaln_kernel, nb=nb),
        out_shape=jax.ShapeDtypeStruct((nb, n), F32),
        grid=(n // ADA_TN,),
        in_specs=[
            pl.BlockSpec((d, nb), lambda j: (0, 0)),
            pl.BlockSpec((d, ADA_TN), lambda j: (0, j)),
            pl.BlockSpec((1, ADA_TN), lambda j: (0, j)),
        ],
        out_specs=pl.BlockSpec((nb, ADA_TN), lambda j: (0, j)),
        scratch_shapes=[pltpu.VMEM((d, nb), F32)],
        compiler_params=pltpu.CompilerParams(
            dimension_semantics=("arbitrary",), vmem_limit_bytes=VMEM_LIMIT),
        name="adaln",
    )(c.T, w, bias.reshape(1, n))


def _ffn_kernel(x_ref, mod_ref, w1_ref, w3_ref, w2_ref, lng_ref, lnb_ref, o_ref,
                u_ref, acc_ref, *, sub):
    j = pl.program_id(1)

    @pl.when(j == 0)
    def _():
        shift = mod_ref[0, 3 * sub:3 * sub + 1, :]
        scale = mod_ref[0, 3 * sub + 1:3 * sub + 2, :]
        u_ref[...] = (_ln(x_ref[...]) * (1.0 + scale) + shift).astype(BF16)

    u = u_ref[...]
    g = _dot(u, w1_ref[...])
    v = _dot(u, w3_ref[...])
    h = (g * _sigmoid(g) * v).astype(BF16)
    contrib = _dot(h, w2_ref[...])

    @pl.when(j == 0)
    def _():
        acc_ref[...] = contrib

    @pl.when(j > 0)
    def _():
        acc_ref[...] += contrib

    @pl.when(j == pl.num_programs(1) - 1)
    def _():
        gate = mod_ref[0, 3 * sub + 2:3 * sub + 3, :]
        r = ALPHA * x_ref[...] + (FFN_WEIGHT * gate) * acc_ref[...]
        o_ref[...] = _ln(r) * lng_ref[...] + lnb_ref[...]


def _ffn(x2d, mod, w13, w2, lng, lnb, *, sub, rows_per_batch):
    m, d = x2d.shape
    f = w2.shape[0]
    nf = f // FFN_TF
    tpb = rows_per_batch // FFN_TM
    return pl.pallas_call(
        functools.partial(_ffn_kernel, sub=sub),
        out_shape=jax.ShapeDtypeStruct((m, d), F32),
        grid=(m // FFN_TM, nf),
        in_specs=[
            pl.BlockSpec((FFN_TM, d), lambda i, j: (i, 0)),
            pl.BlockSpec((1, 3 * N_SUB, d), lambda i, j: (i // tpb, 0, 0)),
            pl.BlockSpec((d, FFN_TF), lambda i, j: (0, j)),
            pl.BlockSpec((d, FFN_TF), lambda i, j: (0, j + nf)),
            pl.BlockSpec((FFN_TF, d), lambda i, j: (j, 0)),
            pl.BlockSpec((1, d), lambda i, j: (0, 0)),
            pl.BlockSpec((1, d), lambda i, j: (0, 0)),
        ],
        out_specs=pl.BlockSpec((FFN_TM, d), lambda i, j: (i, 0)),
        scratch_shapes=[pltpu.VMEM((FFN_TM, d), BF16), pltpu.VMEM((FFN_TM, d), F32)],
        compiler_params=pltpu.CompilerParams(
            dimension_semantics=("parallel", "arbitrary"), vmem_limit_bytes=VMEM_LIMIT),
        name=f"ffn{sub}",
    )(x2d, mod, w13, w13, w2, lng, lnb)


def _inproj_kernel(x_ref, mod_ref, w_ref, b_ref, wg_ref, bg_ref, p_ref, g_ref, u_ref):
    n = pl.program_id(1)

    @pl.when(n == 0)
    def _():
        shift = mod_ref[0, 3:4, :]
        scale = mod_ref[0, 4:5, :]
        u = (_ln(x_ref[...]) * (1.0 + scale) + shift).astype(BF16)
        u_ref[...] = u
        g_ref[...] = _dot(u, wg_ref[...]) + bg_ref[...]

    p_ref[...] = (_dot(u_ref[...], w_ref[...]) + b_ref[...]).astype(BF16)


def _inproj(x2d, mod, w_main, b_main, w_gate, b_gate, *, rows_per_batch):
    m, d = x2d.shape
    nmain = w_main.shape[1]
    tpb = rows_per_batch // PROJ_TM
    return pl.pallas_call(
        _inproj_kernel,
        out_shape=(jax.ShapeDtypeStruct((m, nmain), BF16),
                   jax.ShapeDtypeStruct((m, LANES), F32)),
        grid=(m // PROJ_TM, nmain // PROJ_TN),
        in_specs=[
            pl.BlockSpec((PROJ_TM, d), lambda i, n: (i, 0)),
            pl.BlockSpec((1, 3 * N_SUB, d), lambda i, n: (i // tpb, 0, 0)),
            pl.BlockSpec((d, PROJ_TN), lambda i, n: (0, n)),
            pl.BlockSpec((1, PROJ_TN), lambda i, n: (0, n)),
            pl.BlockSpec((d, LANES), lambda i, n: (0, 0)),
            pl.BlockSpec((1, LANES), lambda i, n: (0, 0)),
        ],
        out_specs=(pl.BlockSpec((PROJ_TM, PROJ_TN), lambda i, n: (i, n)),
                   pl.BlockSpec((PROJ_TM, LANES), lambda i, n: (i, 0))),
        scratch_shapes=[pltpu.VMEM((PROJ_TM, d), BF16)],
        compiler_params=pltpu.CompilerParams(
            dimension_semantics=("parallel", "arbitrary"), vmem_limit_bytes=VMEM_LIMIT),
        name="inproj",
    )(x2d, mod, w_main, b_main, w_gate, b_gate)


def _mlstm_kernel(q_ref, k_ref, v_ref, o_ref, g_ref, ng_ref, out_ref,
                  c_ref, n_ref, m_ref, *, heads, dh):
    L = q_ref.shape[1]
    scale = dh ** -0.5

    @pl.when(pl.program_id(1) == 0)
    def _():
        c_ref[...] = jnp.zeros_like(c_ref)
        n_ref[...] = jnp.zeros_like(n_ref)
        m_ref[...] = jnp.zeros_like(m_ref)

    gates = g_ref[0]
    lsg = _log_sigmoid(gates)
    row = lax.broadcasted_iota(jnp.int32, (L, L), 0)
    col = lax.broadcasted_iota(jnp.int32, (L, L), 1)
    lower = row >= col
    ltri = jnp.where(lower, 1.0, 0.0).astype(F32)
    utri = jnp.where(row <= col, 1.0, 0.0).astype(F32)
    b_cols = jnp.dot(ltri, lsg, preferred_element_type=F32,
                     precision=lax.Precision.HIGHEST)
    gates_t = gates.T
    lsg_t = _log_sigmoid(gates_t[0:SUBLANES, :])
    b_rows = jnp.dot(lsg_t, utri, preferred_element_type=F32,
                     precision=lax.Precision.HIGHEST)

    for h in range(heads):
        sl = slice(h * dh, (h + 1) * dh)
        q = q_ref[0, :, sl]
        k = k_ref[0, :, sl]
        v = v_ref[0, :, sl]
        b_col = b_cols[:, heads + h:heads + h + 1]
        b_row = b_rows[heads + h:heads + h + 1, :]
        li_col = gates[:, h:h + 1]
        li_row = gates_t[h:h + 1, :]
        m_prev = m_ref[h:h + 1, 0:1]

        d = jnp.where(lower, b_col - b_row + li_row, -jnp.inf)
        inter = b_col + m_prev
        m_t = jnp.maximum(inter, jnp.max(d, axis=-1, keepdims=True))
        w_intra = jnp.exp(d - m_t)
        w_inter = jnp.exp(inter - m_t)
        s = lax.dot_general(q, k, (((1,), (1,)), ((), ())),
                            preferred_element_type=F32) * scale * w_intra
        c_prev = c_ref[h]
        num = w_inter * _dot(q, c_prev.astype(BF16)) + _dot(s.astype(BF16), v)
        qf = q.astype(F32)
        den = (w_inter * jnp.sum(qf * n_ref[h:h + 1, :], axis=-1, keepdims=True)
               + jnp.sum(s, axis=-1, keepdims=True))
        hh = num / jnp.maximum(jnp.abs(den), jnp.exp(-m_t))

        b_last = b_row[:, L - 1:L]
        w_log = b_last - b_col + li_col
        m_new = jnp.maximum(b_last + m_prev, jnp.max(w_log, axis=0, keepdims=True))
        decay = jnp.exp(b_last + m_prev - m_new)
        w_s = jnp.exp(w_log - m_new)
        ks = k.astype(F32) * (scale * w_s)
        c_ref[h] = decay * c_prev + _dot(ks.T.astype(BF16), v)
        n_ref[h:h + 1, :] = decay * n_ref[h:h + 1, :] + jnp.sum(ks, axis=0, keepdims=True)
        m_ref[h:h + 1, :] = jnp.broadcast_to(m_new, (1, LANES))

        hn = _ln(hh) * ng_ref[h:h + 1, :]
        og = o_ref[0, :, sl].astype(F32)
        out_ref[0, :, sl] = (hn * _sigmoid(og)).astype(BF16)


def _mlstm(p3, gates3, norm_g, *, heads, dh):
    bsz, s, _ = p3.shape
    dm = heads * dh
    L = MLSTM_L
    blk = lambda off: pl.BlockSpec((1, L, dm), lambda b, c, off=off: (b, c, off))
    return pl.pallas_call(
        functools.partial(_mlstm_kernel, heads=heads, dh=dh),
        out_shape=jax.ShapeDtypeStruct((bsz, s, dm), BF16),
        grid=(bsz, s // L),
        in_specs=[blk(0), blk(1), blk(2), blk(3),
                  pl.BlockSpec((1, L, LANES), lambda b, c: (b, c, 0)),
                  pl.BlockSpec((heads, dh), lambda b, c: (0, 0))],
        out_specs=pl.BlockSpec((1, L, dm), lambda b, c: (b, c, 0)),
        scratch_shapes=[pltpu.VMEM((heads, dh, dh), F32),
                        pltpu.VMEM((SUBLANES, dh), F32),
                        pltpu.VMEM((SUBLANES, LANES), F32)],
        compiler_params=pltpu.CompilerParams(
            dimension_semantics=("parallel", "arbitrary"), vmem_limit_bytes=VMEM_LIMIT),
        name="mlstm",
    )(p3, p3, p3, p3, gates3, norm_g)


def _rglru_kernel(x_ref, g_ref, cw_ref, cb_ref, wa_ref, ba_ref, wx_ref, bx_ref, lam_ref,
                  out_ref, xp_ref, a_ref, u_ref, h_ref, hc_ref):
    T = x_ref.shape[1]
    cbw = x_ref.shape[2]
    PAD = SUBLANES

    @pl.when(pl.program_id(2) == 0)
    def _():
        xp_ref[0:PAD, :] = jnp.zeros((PAD, cbw), F32)
        hc_ref[...] = jnp.zeros_like(hc_ref)

    xp_ref[PAD:PAD + T, :] = x_ref[0].astype(F32)
    xc = cb_ref[...] + cw_ref[CONV_WIDTH - 1:CONV_WIDTH, :] * xp_ref[PAD:PAD + T, :]
    for tap in range(CONV_WIDTH - 1):
        off = PAD - (CONV_WIDTH - 1) + tap
        xc = xc + cw_ref[tap:tap + 1, :] * xp_ref[off:off + T, :]
    xp_ref[0:PAD, :] = xp_ref[T:T + PAD, :]

    sp = _softplus(-lam_ref[...])
    for nb in range(cbw // RG_BLOCK_DIM):
        sl = slice(nb * RG_BLOCK_DIM, (nb + 1) * RG_BLOCK_DIM)
        xb = xc[:, sl]
        xb16 = xb.astype(BF16)
        r = _sigmoid(_dot(xb16, wa_ref[nb]) + ba_ref[:, sl])
        i = _sigmoid(_dot(xb16, wx_ref[nb]) + bx_ref[:, sl])
        log_a = (-RG_C) * r * sp[:, sl]
        a = jnp.exp(log_a)
        a_ref[:, sl] = a
        u_ref[:, sl] = jnp.sqrt(-jnp.tanh(log_a) * (a * a + 1.0)) * (i * xb)

    def body(t, hprev):
        r0 = pl.multiple_of(t * SUBLANES, SUBLANES)
        a8 = a_ref[pl.ds(r0, SUBLANES), :]
        u8 = u_ref[pl.ds(r0, SUBLANES), :]
        rows = []
        hcur = hprev
        for r in range(SUBLANES):
            hcur = a8[r:r + 1, :] * hcur + u8[r:r + 1, :]
            rows.append(hcur)
        h_ref[pl.ds(r0, SUBLANES), :] = jnp.concatenate(rows, axis=0)
        return hcur

    hlast = lax.fori_loop(0, T // SUBLANES, body, hc_ref[0:1, :])
    hc_ref[0:1, :] = hlast
    out_ref[0] = (h_ref[...] * _gelu_tanh(g_ref[0].astype(F32))).astype(BF16)


def _rglru(p3, conv_w, conv_b, wa, ba, wx, bx, lam, *, x_off, g_off):
    bsz, s, _ = p3.shape
    dr = conv_w.shape[1]
    T, CB = RG_T, RG_CB
    nbk = CB // RG_BLOCK_DIM
    xo, go = x_off // CB, g_off // CB
    vec = lambda: pl.BlockSpec((1, CB), lambda b, c, t: (0, c))
    return pl.pallas_call(
        _rglru_kernel,
        out_shape=jax.ShapeDtypeStruct((bsz, s, dr), BF16),
        grid=(bsz, dr // CB, s // T),
        in_specs=[
            pl.BlockSpec((1, T, CB), lambda b, c, t: (b, t, xo + c)),
            pl.BlockSpec((1, T, CB), lambda b, c, t: (b, t, go + c)),
            pl.BlockSpec((CONV_WIDTH, CB), lambda b, c, t: (0, c)),
            vec(),
            pl.BlockSpec((nbk, RG_BLOCK_DIM, RG_BLOCK_DIM), lambda b, c, t: (c, 0, 0)),
            vec(),
            pl.BlockSpec((nbk, RG_BLOCK_DIM, RG_BLOCK_DIM), lambda b, c, t: (c, 0, 0)),
            vec(),
            vec(),
        ],
        out_specs=pl.BlockSpec((1, T, CB), lambda b, c, t: (b, t, c)),
        scratch_shapes=[pltpu.VMEM((T + SUBLANES, CB), F32),
                        pltpu.VMEM((T, CB), F32),
                        pltpu.VMEM((T, CB), F32),
                        pltpu.VMEM((T, CB), F32),
                        pltpu.VMEM((SUBLANES, CB), F32)],
        compiler_params=pltpu.CompilerParams(
            dimension_semantics=("parallel", "parallel", "arbitrary"),
            vmem_limit_bytes=VMEM_LIMIT),
        name="rglru",
    )(p3, p3, conv_w, conv_b, wa, ba, wx, bx, lam)


def _outproj_kernel(x_ref, hm_ref, hr_ref, mod_ref, wt_ref, wb_ref, lng_ref, lnb_ref, o_ref):
    y = _dot(hm_ref[...], wt_ref[...]) + _dot(hr_ref[...], wb_ref[...])
    gate = mod_ref[0, 5:6, :]
    r = ALPHA * x_ref[...] + gate * y
    o_ref[...] = _ln(r) * lng_ref[...] + lnb_ref[...]


def _outproj(x2d, hm, hr, mod, w_top, w_bot, lng, lnb, *, rows_per_batch):
    m, d = x2d.shape
    dm = hm.shape[1]
    dr = hr.shape[1]
    tpb = rows_per_batch // OUT_TM
    return pl.pallas_call(
        _outproj_kernel,
        out_shape=jax.ShapeDtypeStruct((m, d), F32),
        grid=(m // OUT_TM,),
        in_specs=[
            pl.BlockSpec((OUT_TM, d), lambda i: (i, 0)),
            pl.BlockSpec((OUT_TM, dm), lambda i: (i, 0)),
            pl.BlockSpec((OUT_TM, dr), lambda i: (i, 0)),
            pl.BlockSpec((1, 3 * N_SUB, d), lambda i: (i // tpb, 0, 0)),
            pl.BlockSpec((dm, d), lambda i: (0, 0)),
            pl.BlockSpec((dr, d), lambda i: (0, 0)),
            pl.BlockSpec((1, d), lambda i: (0, 0)),
            pl.BlockSpec((1, d), lambda i: (0, 0)),
        ],
        out_specs=pl.BlockSpec((OUT_TM, d), lambda i: (i, 0)),
        compiler_params=pltpu.CompilerParams(
            dimension_semantics=("parallel",), vmem_limit_bytes=VMEM_LIMIT),
        name="outproj",
    )(x2d, hm, hr, mod, w_top, w_bot, lng, lnb)


def kernel(x, c, w_ada, b_ada, ffn1_w13, ffn1_w2, w_in, b_in, mlstm_norm_g, rg_conv_w,
           rg_conv_b, rg_wa, rg_ba, rg_wx, rg_bx, rg_lambda, w_out, ffn2_w13, ffn2_w2,
           ln_g, ln_b):
    bsz, s, d = x.shape
    depth = w_ada.shape[0]
    heads = mlstm_norm_g.shape[1]
    dh = mlstm_norm_g.shape[2]
    dm = heads * dh
    dr = rg_conv_w.shape[2]
    m = bsz * s
    x2d = x.reshape(m, d)
    for l in range(depth):
        mod = _adaln(c, w_ada[l], b_ada[l]).reshape(bsz, 3 * N_SUB, d)

        x2d = _ffn(x2d, mod, ffn1_w13[l].astype(BF16), ffn1_w2[l].astype(BF16),
                   ln_g[l, 0:1], ln_b[l, 0:1], sub=0, rows_per_batch=s)

        wl, bl = w_in[l], b_in[l]
        g0 = 4 * dm
        r0 = g0 + 2 * heads
        w_main = jnp.concatenate([wl[:, :g0], wl[:, r0:]], axis=1).astype(BF16)
        b_main = jnp.concatenate([bl[:g0], bl[r0:]]).reshape(1, -1)
        w_gate = jnp.pad(wl[:, g0:r0], ((0, 0), (0, LANES - 2 * heads))).astype(BF16)
        b_gate = jnp.pad(bl[g0:r0], (0, LANES - 2 * heads)).reshape(1, LANES)
        p, gates = _inproj(x2d, mod, w_main, b_main, w_gate, b_gate, rows_per_batch=s)
        p3 = p.reshape(bsz, s, -1)
        gates3 = gates.reshape(bsz, s, LANES)

        h_m = _mlstm(p3, gates3, mlstm_norm_g[l], heads=heads, dh=dh)
        h_r = _rglru(p3, rg_conv_w[l], rg_conv_b[l].reshape(1, dr),
                     rg_wa[l].astype(BF16), rg_ba[l].reshape(1, dr),
                     rg_wx[l].astype(BF16), rg_bx[l].reshape(1, dr),
                     rg_lambda[l].reshape(1, dr), x_off=g0, g_off=g0 + dr)

        wo = w_out[l].astype(BF16)
        x2d = _outproj(x2d, h_m.reshape(m, dm), h_r.reshape(m, dr), mod,
                       wo[:dm], wo[dm:], ln_g[l, 1:2], ln_b[l, 1:2], rows_per_batch=s)

        x2d = _ffn(x2d, mod, ffn2_w13[l].astype(BF16), ffn2_w2[l].astype(BF16),
                   ln_g[l, 2:3], ln_b[l, 2:3], sub=2, rows_per_batch=s)
    return x2d.reshape(bsz, s, d)
```

```python
import functools

import jax
import jax.numpy as jnp
from jax import lax
from jax.experimental import pallas as pl
from jax.experimental.pallas import tpu as pltpu

F32 = jnp.float32
BF16 = jnp.bfloat16

N_SUB = 3
MLSTM_HEADS = 4
RG_BLOCK_DIM = 128
CONV_WIDTH = 4
RG_C = 8.0
FFN_WEIGHT = 0.5
DEPTH = 1
ALPHA = float((2 * DEPTH) ** 0.25)
LN_EPS = 1e-5

LANES = 128
SUBLANES = 8
VMEM_LIMIT = 48 * 1024 * 1024

ADA_TN = 1024
ADA_RC = 256
FFN_TM = 512
FFN_TF = 512
PROJ_TM = 1024
PROJ_TN = 1024
MLSTM_L = 256
RG_T = 1024
RG_CB = 512
OUT_TM = 512


def _ln(x):
    mu = jnp.mean(x, axis=-1, keepdims=True)
    xc = x - mu
    var = jnp.mean(xc * xc, axis=-1, keepdims=True)
    return xc * lax.rsqrt(var + LN_EPS)


def _sigmoid(x):
    return 1.0 / (1.0 + jnp.exp(-x))


def _log_sigmoid(x):
    return jnp.minimum(x, 0.0) - jnp.log1p(jnp.exp(-jnp.abs(x)))


def _softplus(x):
    return jnp.maximum(x, 0.0) + jnp.log1p(jnp.exp(-jnp.abs(x)))


def _gelu_tanh(x):
    c = 0.7978845608028654
    return 0.5 * x * (1.0 + jnp.tanh(c * (x + 0.044715 * (x * x * x))))


def _dot(a, b):
    return jnp.dot(a, b, preferred_element_type=F32)


def _adaln_kernel(ct_ref, w_ref, b_ref, o_ref, s_ref, *, nb):
    ct = ct_ref[...]
    s_ref[...] = ct * _sigmoid(ct)
    d = w_ref.shape[0]
    tn = w_ref.shape[1]

    def body(i, accs):
        r0 = pl.multiple_of(i * ADA_RC, ADA_RC)
        wblk = w_ref[pl.ds(r0, ADA_RC), :]
        sblk = s_ref[pl.ds(r0, ADA_RC), :]
        out = []
        for b in range(nb):
            p = wblk * sblk[:, b:b + 1]
            out.append(accs[b] + jnp.sum(p.reshape(ADA_RC // SUBLANES, SUBLANES, tn), axis=0))
        return tuple(out)

    accs = lax.fori_loop(0, d // ADA_RC, body,
                         tuple(jnp.zeros((SUBLANES, tn), F32) for _ in range(nb)))
    for b in range(nb):
        o_ref[b:b + 1, :] = jnp.sum(accs[b], axis=0, keepdims=True) + b_ref[...]


def _adaln(c, w, bias):
    nb, d = c.shape
    n = w.shape[1]
    return pl.pallas_call(
        functools.partial(_adaln_kernel, nb=nb),
        out_shape=jax.ShapeDtypeStruct((nb, n), F32),
        grid=(n // ADA_TN,),
        in_specs=[
            pl.BlockSpec((d, nb), lambda j: (0, 0)),
            pl.BlockSpec((d, ADA_TN), lambda j: (0, j)),
            pl.BlockSpec((1, ADA_TN), lambda j: (0, j)),
        ],
        out_specs=pl.BlockSpec((nb, ADA_TN), lambda j: (0, j)),
        scratch_shapes=[pltpu.VMEM((d, nb), F32)],
        compiler_params=pltpu.CompilerParams(
            dimension_semantics=("arbitrary",), vmem_limit_bytes=VMEM_LIMIT),
        name="adaln",
    )(c.T, w, bias.reshape(1, n))


def _ffn_kernel(x_ref, mod_ref, w1_ref, w3_ref, w2_ref, lng_ref, lnb_ref, o_ref,
                u_ref, *, sub):
    j = pl.program_id(1)

    @pl.when(j == 0)
    def _():
        shift = mod_ref[0, 3 * sub:3 * sub + 1, :]
        scale = mod_ref[0, 3 * sub + 1:3 * sub + 2, :]
        u_ref[...] = (_ln(x_ref[...]) * (1.0 + scale) + shift).astype(BF16)
        o_ref[...] = jnp.zeros_like(o_ref)

    u = u_ref[...]
    g = _dot(u, w1_ref[...])
    v = _dot(u, w3_ref[...])
    h = (g * _sigmoid(g) * v).astype(BF16)
    o_ref[...] += _dot(h, w2_ref[...])

    @pl.when(j == pl.num_programs(1) - 1)
    def _():
        gate = mod_ref[0, 3 * sub + 2:3 * sub + 3, :]
        r = ALPHA * x_ref[...] + (FFN_WEIGHT * gate) * o_ref[...]
        o_ref[...] = _ln(r) * lng_ref[...] + lnb_ref[...]


def _ffn(x2d, mod, w13, w2, lng, lnb, *, sub, rows_per_batch):
    m, d = x2d.shape
    f = w2.shape[0]
    nf = f // FFN_TF
    tpb = rows_per_batch // FFN_TM
    return pl.pallas_call(
        functools.partial(_ffn_kernel, sub=sub),
        out_shape=jax.ShapeDtypeStruct((m, d), F32),
        grid=(m // FFN_TM, nf),
        in_specs=[
            pl.BlockSpec((FFN_TM, d), lambda i, j: (i, 0)),
            pl.BlockSpec((1, 3 * N_SUB, d), lambda i, j: (i // tpb, 0, 0)),
            pl.BlockSpec((d, FFN_TF), lambda i, j: (0, j)),
            pl.BlockSpec((d, FFN_TF), lambda i, j: (0, j + nf)),
            pl.BlockSpec((FFN_TF, d), lambda i, j: (j, 0)),
            pl.BlockSpec((1, d), lambda i, j: (0, 0)),
            pl.BlockSpec((1, d), lambda i, j: (0, 0)),
        ],
        out_specs=pl.BlockSpec((FFN_TM, d), lambda i, j: (i, 0)),
        scratch_shapes=[pltpu.VMEM((FFN_TM, d), BF16)],
        compiler_params=pltpu.CompilerParams(
            dimension_semantics=("parallel", "arbitrary"), vmem_limit_bytes=VMEM_LIMIT),
        name=f"ffn{sub}",
    )(x2d, mod, w13, w13, w2, lng, lnb)


def _inproj_kernel(x_ref, mod_ref, wa_ref, ba_ref, wb_ref, bb_ref, wg_ref, bg_ref,
                   p_ref, g_ref, u_ref, *, na):
    n = pl.program_id(1)

    @pl.when(n == 0)
    def _():
        shift = mod_ref[0, 3:4, :]
        scale = mod_ref[0, 4:5, :]
        u = (_ln(x_ref[...]) * (1.0 + scale) + shift).astype(BF16)
        u_ref[...] = u
        g_ref[...] = _dot(u, wg_ref[...]) + bg_ref[...]

    @pl.when(n < na)
    def _():
        p_ref[...] = (_dot(u_ref[...], wa_ref[...]) + ba_ref[...]).astype(BF16)

    @pl.when(n >= na)
    def _():
        p_ref[...] = (_dot(u_ref[...], wb_ref[...]) + bb_ref[...]).astype(BF16)


def _inproj(x2d, mod, w_a, b_a, w_b, b_b, w_gate, b_gate, *, rows_per_batch):
    m, d = x2d.shape
    na = w_a.shape[1] // PROJ_TN
    nb = w_b.shape[1] // PROJ_TN
    tpb = rows_per_batch // PROJ_TM
    return pl.pallas_call(
        functools.partial(_inproj_kernel, na=na),
        out_shape=(jax.ShapeDtypeStruct((m, (na + nb) * PROJ_TN), BF16),
                   jax.ShapeDtypeStruct((m, LANES), F32)),
        grid=(m // PROJ_TM, na + nb),
        in_specs=[
            pl.BlockSpec((PROJ_TM, d), lambda i, n: (i, 0)),
            pl.BlockSpec((1, 3 * N_SUB, d), lambda i, n: (i // tpb, 0, 0)),
            pl.BlockSpec((d, PROJ_TN), lambda i, n: (0, jnp.minimum(n, na - 1))),
            pl.BlockSpec((1, PROJ_TN), lambda i, n: (0, jnp.minimum(n, na - 1))),
            pl.BlockSpec((d, PROJ_TN), lambda i, n: (0, jnp.maximum(n - na, 0))),
            pl.BlockSpec((1, PROJ_TN), lambda i, n: (0, jnp.maximum(n - na, 0))),
            pl.BlockSpec((d, LANES), lambda i, n: (0, 0)),
            pl.BlockSpec((1, LANES), lambda i, n: (0, 0)),
        ],
        out_specs=(pl.BlockSpec((PROJ_TM, PROJ_TN), lambda i, n: (i, n)),
                   pl.BlockSpec((PROJ_TM, LANES), lambda i, n: (i, 0))),
        scratch_shapes=[pltpu.VMEM((PROJ_TM, d), BF16)],
        compiler_params=pltpu.CompilerParams(
            dimension_semantics=("parallel", "arbitrary"), vmem_limit_bytes=VMEM_LIMIT),
        name="inproj",
    )(x2d, mod, w_a, b_a, w_b, b_b, w_gate, b_gate)


def _mlstm_kernel(q_ref, k_ref, v_ref, o_ref, g_ref, ng_ref, out_ref,
                  c_ref, n_ref, m_ref, *, heads, dh):
    L = q_ref.shape[1]
    scale = dh ** -0.5

    @pl.when(pl.program_id(1) == 0)
    def _():
        c_ref[...] = jnp.zeros_like(c_ref)
        n_ref[...] = jnp.zeros_like(n_ref)
        m_ref[...] = jnp.zeros_like(m_ref)

    gates = g_ref[0]
    lsg = _log_sigmoid(gates)
    row = lax.broadcasted_iota(jnp.int32, (L, L), 0)
    col = lax.broadcasted_iota(jnp.int32, (L, L), 1)
    lower = row >= col
    ltri = jnp.where(lower, 1.0, 0.0).astype(F32)
    utri = jnp.where(row <= col, 1.0, 0.0).astype(F32)
    b_cols = jnp.dot(ltri, lsg, preferred_element_type=F32,
                     precision=lax.Precision.HIGHEST)
    gates_t = gates.T
    lsg_t = _log_sigmoid(gates_t[0:SUBLANES, :])
    b_rows = jnp.dot(lsg_t, utri, preferred_element_type=F32,
                     precision=lax.Precision.HIGHEST)

    for h in range(heads):
        sl = slice(h * dh, (h + 1) * dh)
        q = q_ref[0, :, sl]
        k = k_ref[0, :, sl]
        v = v_ref[0, :, sl]
        b_col = b_cols[:, heads + h:heads + h + 1]
        b_row = b_rows[heads + h:heads + h + 1, :]
        li_col = gates[:, h:h + 1]
        li_row = gates_t[h:h + 1, :]
        m_prev = m_ref[h:h + 1, 0:1]

        d = jnp.where(lower, b_col - b_row + li_row, -jnp.inf)
        inter = b_col + m_prev
        m_t = jnp.maximum(inter, jnp.max(d, axis=-1, keepdims=True))
        w_intra = jnp.exp(d - m_t)
        w_inter = jnp.exp(inter - m_t)
        s = lax.dot_general(q, k, (((1,), (1,)), ((), ())),
                            preferred_element_type=F32) * scale * w_intra
        c_prev = c_ref[h]
        num = w_inter * _dot(q, c_prev.astype(BF16)) + _dot(s.astype(BF16), v)
        qf = q.astype(F32)
        den = (w_inter * jnp.sum(qf * n_ref[h:h + 1, :], axis=-1, keepdims=True)
               + jnp.sum(s, axis=-1, keepdims=True))
        hh = num / jnp.maximum(jnp.abs(den), jnp.exp(-m_t))

        b_last = b_row[:, L - 1:L]
        w_log = b_last - b_col + li_col
        m_new = jnp.maximum(b_last + m_prev, jnp.max(w_log, axis=0, keepdims=True))
        decay = jnp.exp(b_last + m_prev - m_new)
        w_s = jnp.exp(w_log - m_new)
        ks = k.astype(F32) * (scale * w_s)
        c_ref[h] = decay * c_prev + _dot(ks.T.astype(BF16), v)
        n_ref[h:h + 1, :] = decay * n_ref[h:h + 1, :] + jnp.sum(ks, axis=0, keepdims=True)
        m_ref[h:h + 1, :] = jnp.broadcast_to(m_new, (1, LANES))

        hn = _ln(hh) * ng_ref[h:h + 1, :]
        og = o_ref[0, :, sl].astype(F32)
        out_ref[0, :, sl] = (hn * _sigmoid(og)).astype(BF16)


def _mlstm(p3, gates3, norm_g, *, heads, dh):
    bsz, s, _ = p3.shape
    dm = heads * dh
    L = MLSTM_L
    blk = lambda off: pl.BlockSpec((1, L, dm), lambda b, c, off=off: (b, c, off))
    return pl.pallas_call(
        functools.partial(_mlstm_kernel, heads=heads, dh=dh),
        out_shape=jax.ShapeDtypeStruct((bsz, s, dm), BF16),
        grid=(bsz, s // L),
        in_specs=[blk(0), blk(1), blk(2), blk(3),
                  pl.BlockSpec((1, L, LANES), lambda b, c: (b, c, 0)),
                  pl.BlockSpec((heads, dh), lambda b, c: (0, 0))],
        out_specs=pl.BlockSpec((1, L, dm), lambda b, c: (b, c, 0)),
        scratch_shapes=[pltpu.VMEM((heads, dh, dh), F32),
                        pltpu.VMEM((SUBLANES, dh), F32),
                        pltpu.VMEM((SUBLANES, LANES), F32)],
        compiler_params=pltpu.CompilerParams(
            dimension_semantics=("parallel", "arbitrary"), vmem_limit_bytes=VMEM_LIMIT),
        name="mlstm",
    )(p3, p3, p3, p3, gates3, norm_g)


def _rglru_kernel(x_ref, g_ref, cw_ref, cb_ref, wa_ref, ba_ref, wx_ref, bx_ref, lam_ref,
                  out_ref, xp_ref, a_ref, u_ref, h_ref, hc_ref):
    T = x_ref.shape[1]
    cbw = x_ref.shape[2]
    PAD = SUBLANES

    @pl.when(pl.program_id(2) == 0)
    def _():
        xp_ref[0:PAD, :] = jnp.zeros((PAD, cbw), F32)
        hc_ref[...] = jnp.zeros_like(hc_ref)

    xp_ref[PAD:PAD + T, :] = x_ref[0].astype(F32)
    xc = cb_ref[...] + cw_ref[CONV_WIDTH - 1:CONV_WIDTH, :] * xp_ref[PAD:PAD + T, :]
    for tap in range(CONV_WIDTH - 1):
        off = PAD - (CONV_WIDTH - 1) + tap
        xc = xc + cw_ref[tap:tap + 1, :] * xp_ref[off:off + T, :]
    xp_ref[0:PAD, :] = xp_ref[T:T + PAD, :]

    sp = _softplus(-lam_ref[...])
    for nb in range(cbw // RG_BLOCK_DIM):
        sl = slice(nb * RG_BLOCK_DIM, (nb + 1) * RG_BLOCK_DIM)
        xb = xc[:, sl]
        xb16 = xb.astype(BF16)
        r = _sigmoid(_dot(xb16, wa_ref[nb]) + ba_ref[:, sl])
        i = _sigmoid(_dot(xb16, wx_ref[nb]) + bx_ref[:, sl])
        log_a = (-RG_C) * r * sp[:, sl]
        a = jnp.exp(log_a)
        a_ref[:, sl] = a
        u_ref[:, sl] = jnp.sqrt(-jnp.tanh(log_a) * (a * a + 1.0)) * (i * xb)

    def body(t, hprev):
        r0 = pl.multiple_of(t * SUBLANES, SUBLANES)
        a8 = a_ref[pl.ds(r0, SUBLANES), :]
        u8 = u_ref[pl.ds(r0, SUBLANES), :]
        rows = []
        hcur = hprev
        for r in range(SUBLANES):
            hcur = a8[r:r + 1, :] * hcur + u8[r:r + 1, :]
            rows.append(hcur)
        h_ref[pl.ds(r0, SUBLANES), :] = jnp.concatenate(rows, axis=0)
        return hcur

    hlast = lax.fori_loop(0, T // SUBLANES, body, hc_ref[0:1, :])
    hc_ref[0:1, :] = hlast
    out_ref[0] = (h_ref[...] * _gelu_tanh(g_ref[0].astype(F32))).astype(BF16)


def _rglru(p3, conv_w, conv_b, wa, ba, wx, bx, lam, *, x_off, g_off):
    bsz, s, _ = p3.shape
    dr = conv_w.shape[1]
    T, CB = RG_T, RG_CB
    nbk = CB // RG_BLOCK_DIM
    xo, go = x_off // CB, g_off // CB
    vec = lambda: pl.BlockSpec((1, CB), lambda b, c, t: (0, c))
    return pl.pallas_call(
        _rglru_kernel,
        out_shape=jax.ShapeDtypeStruct((bsz, s, dr), BF16),
        grid=(bsz, dr // CB, s // T),
        in_specs=[
            pl.BlockSpec((1, T, CB), lambda b, c, t: (b, t, xo + c)),
            pl.BlockSpec((1, T, CB), lambda b, c, t: (b, t, go + c)),
            pl.BlockSpec((CONV_WIDTH, CB), lambda b, c, t: (0, c)),
            vec(),
            pl.BlockSpec((nbk, RG_BLOCK_DIM, RG_BLOCK_DIM), lambda b, c, t: (c, 0, 0)),
            vec(),
            pl.BlockSpec((nbk, RG_BLOCK_DIM, RG_BLOCK_DIM), lambda b, c, t: (c, 0, 0)),
            vec(),
            vec(),
        ],
        out_specs=pl.BlockSpec((1, T, CB), lambda b, c, t: (b, t, c)),
        scratch_shapes=[pltpu.VMEM((T + SUBLANES, CB), F32),
                        pltpu.VMEM((T, CB), F32),
                        pltpu.VMEM((T, CB), F32),
                        pltpu.VMEM((T, CB), F32),
                        pltpu.VMEM((SUBLANES, CB), F32)],
        compiler_params=pltpu.CompilerParams(
            dimension_semantics=("parallel", "parallel", "arbitrary"),
            vmem_limit_bytes=VMEM_LIMIT),
        name="rglru",
    )(p3, p3, conv_w, conv_b, wa, ba, wx, bx, lam)


def _outproj_kernel(x_ref, hm_ref, hr_ref, mod_ref, wt_ref, wb_ref, lng_ref, lnb_ref, o_ref):
    y = _dot(hm_ref[...], wt_ref[...]) + _dot(hr_ref[...], wb_ref[...])
    gate = mod_ref[0, 5:6, :]
    r = ALPHA * x_ref[...] + gate * y
    o_ref[...] = _ln(r) * lng_ref[...] + lnb_ref[...]


def _outproj(x2d, hm, hr, mod, w_out, lng, lnb, *, rows_per_batch):
    m, d = x2d.shape
    dm = hm.shape[1]
    dr = hr.shape[1]
    assert dm == dr and w_out.shape == (dm + dr, d)
    tpb = rows_per_batch // OUT_TM
    return pl.pallas_call(
        _outproj_kernel,
        out_shape=jax.ShapeDtypeStruct((m, d), F32),
        grid=(m // OUT_TM,),
        in_specs=[
            pl.BlockSpec((OUT_TM, d), lambda i: (i, 0)),
            pl.BlockSpec((OUT_TM, dm), lambda i: (i, 0)),
            pl.BlockSpec((OUT_TM, dr), lambda i: (i, 0)),
            pl.BlockSpec((1, 3 * N_SUB, d), lambda i: (i // tpb, 0, 0)),
            pl.BlockSpec((dm, d), lambda i: (0, 0)),
            pl.BlockSpec((dr, d), lambda i: (1, 0)),
            pl.BlockSpec((1, d), lambda i: (0, 0)),
            pl.BlockSpec((1, d), lambda i: (0, 0)),
        ],
        out_specs=pl.BlockSpec((OUT_TM, d), lambda i: (i, 0)),
        compiler_params=pltpu.CompilerParams(
            dimension_semantics=("parallel",), vmem_limit_bytes=VMEM_LIMIT),
        name="outproj",
    )(x2d, hm, hr, mod, w_out, w_out, lng, lnb)


def kernel(x, c, w_ada, b_ada, ffn1_w13, ffn1_w2, w_in, b_in, mlstm_norm_g, rg_conv_w,
           rg_conv_b, rg_wa, rg_ba, rg_wx, rg_bx, rg_lambda, w_out, ffn2_w13, ffn2_w2,
           ln_g, ln_b):
    bsz, s, d = x.shape
    depth = w_ada.shape[0]
    heads = mlstm_norm_g.shape[1]
    dh = mlstm_norm_g.shape[2]
    dm = heads * dh
    dr = rg_conv_w.shape[2]
    m = bsz * s
    x2d = x.reshape(m, d)
    for l in range(depth):
        mod = _adaln(c, w_ada[l], b_ada[l]).reshape(bsz, 3 * N_SUB, d)

        x2d = _ffn(x2d, mod, ffn1_w13[l].astype(BF16), ffn1_w2[l].astype(BF16),
                   ln_g[l, 0:1], ln_b[l, 0:1], sub=0, rows_per_batch=s)

        wl, bl = w_in[l], b_in[l]
        g0 = 4 * dm
        r0 = g0 + 2 * heads
        w_gate = jnp.pad(wl[:, g0:r0], ((0, 0), (0, LANES - 2 * heads))).astype(BF16)
        b_gate = jnp.pad(bl[g0:r0], (0, LANES - 2 * heads)).reshape(1, LANES)
        p, gates = _inproj(x2d, mod, wl[:, :g0].astype(BF16), bl[:g0].reshape(1, -1),
                           wl[:, r0:].astype(BF16), bl[r0:].reshape(1, -1),
                           w_gate, b_gate, rows_per_batch=s)
        p3 = p.reshape(bsz, s, -1)
        gates3 = gates.reshape(bsz, s, LANES)

        h_m = _mlstm(p3, gates3, mlstm_norm_g[l], heads=heads, dh=dh)
        h_r = _rglru(p3, rg_conv_w[l], rg_conv_b[l].reshape(1, dr),
                     rg_wa[l].astype(BF16), rg_ba[l].reshape(1, dr),
                     rg_wx[l].astype(BF16), rg_bx[l].reshape(1, dr),
                     rg_lambda[l].reshape(1, dr), x_off=g0, g_off=g0 + dr)

        x2d = _outproj(x2d, h_m.reshape(m, dm), h_r.reshape(m, dr), mod,
                       w_out[l].astype(BF16), ln_g[l, 1:2], ln_b[l, 1:2], rows_per_batch=s)

        x2d = _ffn(x2d, mod, ffn2_w13[l].astype(BF16), ffn2_w2[l].astype(BF16),
                   ln_g[l, 2:3], ln_b[l, 2:3], sub=2, rows_per_batch=s)
    return x2d.reshape(bsz, s, d)
```

```python
import functools
import math

import jax
import jax.numpy as jnp
from jax import lax
from jax.experimental import pallas as pl
from jax.experimental.pallas import tpu as pltpu

F32 = jnp.float32
BF16 = jnp.bfloat16

N_SUB = 3
MLSTM_HEADS = 4
RG_BLOCK_DIM = 128
CONV_WIDTH = 4
RG_C = 8.0
FFN_WEIGHT = 0.5
DEPTH = 1
ALPHA = float((2 * DEPTH) ** 0.25)
LN_EPS = 1e-5

LANES = 128
SUBLANES = 8
VMEM_LIMIT = 48 * 1024 * 1024

ADA_TN = 1024
ADA_RC = 256
FFN_TM = 512
FFN_TF = 512
PROJ_TM = 512
PROJ_TN = 1024
MLSTM_L = 256
MLSTM_GATE_CHUNKS = 8
RG_T = 512
OUT_TM = 512


def _ln(x):
    mu = jnp.mean(x, axis=-1, keepdims=True)
    xc = x - mu
    var = jnp.mean(xc * xc, axis=-1, keepdims=True)
    return xc * lax.rsqrt(var + LN_EPS)


def _sigmoid(x):
    return 1.0 / (1.0 + jnp.exp(-x))


def _log_sigmoid(x):
    return jnp.minimum(x, 0.0) - jnp.log1p(jnp.exp(-jnp.abs(x)))


def _softplus(x):
    return jnp.maximum(x, 0.0) + jnp.log1p(jnp.exp(-jnp.abs(x)))


def _gelu_tanh(x):
    c = 0.7978845608028654
    hx = 0.5 * x
    return hx + hx * jnp.tanh(x * (c + (c * 0.044715) * (x * x)))


def _sqrt_nonneg(w):
    return jnp.where(w > 0.0, w * lax.rsqrt(w), 0.0)


def _dot(a, b):
    return jnp.dot(a, b, preferred_element_type=F32)


def _adaln_kernel(ct_ref, w_ref, b_ref, o_ref, s_ref, *, nb):
    ct = ct_ref[...]
    s_ref[...] = ct * _sigmoid(ct)
    d = w_ref.shape[0]
    tn = w_ref.shape[1]

    def body(i, accs):
        r0 = pl.multiple_of(i * ADA_RC, ADA_RC)
        wblk = w_ref[pl.ds(r0, ADA_RC), :]
        sblk = s_ref[pl.ds(r0, ADA_RC), :]
        out = []
        for b in range(nb):
            p = wblk * sblk[:, b:b + 1]
            out.append(accs[b] + jnp.sum(p.reshape(ADA_RC // SUBLANES, SUBLANES, tn), axis=0))
        return tuple(out)

    accs = lax.fori_loop(0, d // ADA_RC, body,
                         tuple(jnp.zeros((SUBLANES, tn), F32) for _ in range(nb)))
    for b in range(nb):
        o_ref[b:b + 1, :] = jnp.sum(accs[b], axis=0, keepdims=True) + b_ref[...]


def _adaln(c, w, bias):
    nb, d = c.shape
    n = w.shape[1]
    return pl.pallas_call(
        functools.partial(_adaln_kernel, nb=nb),
        out_shape=jax.ShapeDtypeStruct((nb, n), F32),
        grid=(n // ADA_TN,),
        in_specs=[
            pl.BlockSpec((d, nb), lambda j: (0, 0)),
            pl.BlockSpec((d, ADA_TN), lambda j: (0, j)),
            pl.BlockSpec((1, ADA_TN), lambda j: (0, j)),
        ],
        out_specs=pl.BlockSpec((nb, ADA_TN), lambda j: (0, j)),
        scratch_shapes=[pltpu.VMEM((d, nb), F32)],
        compiler_params=pltpu.CompilerParams(
            dimension_semantics=("arbitrary",), vmem_limit_bytes=VMEM_LIMIT),
        name="adaln",
    )(c.T, w, bias.reshape(1, n))


def _ffn_kernel(x_ref, mod_ref, w1_ref, w3_ref, w2_ref, lng_ref, lnb_ref, o_ref,
                u_ref, *, sub):
    j = pl.program_id(1)

    @pl.when(j == 0)
    def _():
        shift = mod_ref[0, 3 * sub:3 * sub + 1, :]
        scale = mod_ref[0, 3 * sub + 1:3 * sub + 2, :]
        u_ref[...] = (_ln(x_ref[...]) * (1.0 + scale) + shift).astype(BF16)
        o_ref[...] = jnp.zeros_like(o_ref)

    u = u_ref[...]
    g = _dot(u, w1_ref[...])
    v = _dot(u, w3_ref[...])
    h = (g * _sigmoid(g) * v).astype(BF16)
    o_ref[...] += _dot(h, w2_ref[...])

    @pl.when(j == pl.num_programs(1) - 1)
    def _():
        gate = mod_ref[0, 3 * sub + 2:3 * sub + 3, :]
        r = ALPHA * x_ref[...] + (FFN_WEIGHT * gate) * o_ref[...]
        o_ref[...] = _ln(r) * lng_ref[...] + lnb_ref[...]


def _ffn(x2d, mod, w13, w2, lng, lnb, *, sub, rows_per_batch):
    m, d = x2d.shape
    f = w2.shape[0]
    nf = f // FFN_TF
    tpb = rows_per_batch // FFN_TM
    return pl.pallas_call(
        functools.partial(_ffn_kernel, sub=sub),
        out_shape=jax.ShapeDtypeStruct((m, d), F32),
        grid=(m // FFN_TM, nf),
        in_specs=[
            pl.BlockSpec((FFN_TM, d), lambda i, j: (i, 0)),
            pl.BlockSpec((1, 3 * N_SUB, d), lambda i, j: (i // tpb, 0, 0)),
            pl.BlockSpec((d, FFN_TF), lambda i, j: (0, j)),
            pl.BlockSpec((d, FFN_TF), lambda i, j: (0, j + nf)),
            pl.BlockSpec((FFN_TF, d), lambda i, j: (j, 0)),
            pl.BlockSpec((1, d), lambda i, j: (0, 0)),
            pl.BlockSpec((1, d), lambda i, j: (0, 0)),
        ],
        out_specs=pl.BlockSpec((FFN_TM, d), lambda i, j: (i, 0)),
        scratch_shapes=[pltpu.VMEM((FFN_TM, d), BF16)],
        compiler_params=pltpu.CompilerParams(
            dimension_semantics=("parallel", "arbitrary"), vmem_limit_bytes=VMEM_LIMIT),
        name=f"ffn{sub}",
    )(x2d, mod, w13, w13, w2, lng, lnb)


def _inproj_kernel(x_ref, mod_ref, wa_ref, ba_ref, wb_ref, bb_ref, wg_ref, bg_ref,
                   p_ref, g_ref, xtm_ref, u_ref, *, na):
    n = pl.program_id(1)
    tm = x_ref.shape[0]

    @pl.when(n == 0)
    def _():
        shift = mod_ref[0, 3:4, :]
        scale = mod_ref[0, 4:5, :]
        u = (_ln(x_ref[...]) * (1.0 + scale) + shift).astype(BF16)
        u_ref[...] = u
        g_ref[...] = _dot(u, wg_ref[...]) + bg_ref[...]

    @pl.when(n < na)
    def _():
        p_ref[...] = (_dot(u_ref[...], wa_ref[...]) + ba_ref[...]).astype(BF16)

    @pl.when(n == na)
    def _():
        xr = _dot(u_ref[...], wb_ref[...]) + bb_ref[...]
        for nb in range(xr.shape[1] // LANES):
            xtm_ref[pl.ds(nb, tm, stride=SUBLANES), :] = xr[:, nb * LANES:(nb + 1) * LANES]

    @pl.when(n > na)
    def _():
        p_ref[...] = (_dot(u_ref[...], wb_ref[...]) + bb_ref[...]).astype(BF16)


def _inproj(x2d, mod, w_a, b_a, w_b, b_b, w_gate, b_gate, *, rows_per_batch):
    m, d = x2d.shape
    na = w_a.shape[1] // PROJ_TN
    nb = w_b.shape[1] // PROJ_TN
    assert nb == 2 and PROJ_TN == SUBLANES * LANES
    tpb = rows_per_batch // PROJ_TM
    p_col = lambda n: n - jnp.where(n >= na, 1, 0)
    return pl.pallas_call(
        functools.partial(_inproj_kernel, na=na),
        out_shape=(jax.ShapeDtypeStruct((m, (na + nb - 1) * PROJ_TN), BF16),
                   jax.ShapeDtypeStruct((m, LANES), F32),
                   jax.ShapeDtypeStruct((m * SUBLANES, LANES), F32)),
        grid=(m // PROJ_TM, na + nb),
        in_specs=[
            pl.BlockSpec((PROJ_TM, d), lambda i, n: (i, 0)),
            pl.BlockSpec((1, 3 * N_SUB, d), lambda i, n: (i // tpb, 0, 0)),
            pl.BlockSpec((d, PROJ_TN), lambda i, n: (0, jnp.minimum(n, na - 1))),
            pl.BlockSpec((1, PROJ_TN), lambda i, n: (0, jnp.minimum(n, na - 1))),
            pl.BlockSpec((d, PROJ_TN), lambda i, n: (0, jnp.maximum(n - na, 0))),
            pl.BlockSpec((1, PROJ_TN), lambda i, n: (0, jnp.maximum(n - na, 0))),
            pl.BlockSpec((d, LANES), lambda i, n: (0, 0)),
            pl.BlockSpec((1, LANES), lambda i, n: (0, 0)),
        ],
        out_specs=(pl.BlockSpec((PROJ_TM, PROJ_TN), lambda i, n: (i, p_col(n))),
                   pl.BlockSpec((PROJ_TM, LANES), lambda i, n: (i, 0)),
                   pl.BlockSpec((PROJ_TM * SUBLANES, LANES), lambda i, n: (i, 0))),
        scratch_shapes=[pltpu.VMEM((PROJ_TM, d), BF16)],
        compiler_params=pltpu.CompilerParams(
            dimension_semantics=("parallel", "arbitrary"), vmem_limit_bytes=VMEM_LIMIT),
        name="inproj",
    )(x2d, mod, w_a, b_a, w_b, b_b, w_gate, b_gate)


def _lane_scan(x, op, fill):
    n = x.shape[1]
    lane = lax.broadcasted_iota(jnp.int32, x.shape, 1)
    k = 1
    while k < n:
        x = op(x, jnp.where(lane >= k, pltpu.roll(x, k, axis=1), fill))
        k *= 2
    return x


def _mlstm_gates_kernel(g_ref, cols_ref, gs_ref, dec_ref, m_ref, *, heads, dh, L):
    G = gs_ref.shape[1]
    log_scale = -0.5 * math.log(dh)

    @pl.when(pl.program_id(1) == 0)
    def _():
        m_ref[...] = jnp.zeros_like(m_ref)

    gi, gf = [], []
    for c in range(G):
        gt = g_ref[0, c * L:(c + 1) * L, :].T[0:SUBLANES, :]
        gi.append(gt)
        gf.append(pltpu.roll(gt, SUBLANES - heads, axis=0))
    gi = jnp.concatenate(gi, axis=0)
    gf = jnp.concatenate(gf, axis=0)
    b = _lane_scan(_log_sigmoid(gf), jnp.add, 0.0)
    g = gi - b
    cm = _lane_scan(g, jnp.maximum, -jnp.inf)

    m_prev = m_ref[...]
    m_prevs, m_lasts = [], []
    for c in range(G):
        rows = slice(c * SUBLANES, (c + 1) * SUBLANES)
        m_last = jnp.maximum(m_prev, cm[rows, L - 1:L])
        m_prevs.append(m_prev)
        m_lasts.append(m_last)
        m_prev = b[rows, L - 1:L] + m_last
    m_ref[...] = m_prev
    tile = lambda xs: jnp.concatenate([jnp.concatenate(xs, axis=0)] * (L // LANES), axis=1)
    m_prev_b = tile(m_prevs)
    m_last_b = tile(m_lasts)
    big_m = jnp.maximum(m_prev_b, cm)
    w_inter = jnp.exp(m_prev_b - big_m)
    e_clamp = jnp.exp(-(b + big_m))
    gs = g + log_scale
    w_state = jnp.exp(gs - m_last_b)
    decay = jnp.exp(m_prev_b - m_last_b)
    pad = jnp.zeros((LANES - 4 * SUBLANES, L), F32)
    for c in range(G):
        rows = slice(c * SUBLANES, (c + 1) * SUBLANES)
        cols_ref[0, c * L:(c + 1) * L, :] = jnp.concatenate(
            [big_m[rows], w_inter[rows], e_clamp[rows], w_state[rows], pad], axis=0).T
        gs_ref[0, c] = gs[rows]
        dec_ref[0, c] = decay[rows, 0:LANES]


def _mlstm_gates(gates3, *, heads, dh, L):
    bsz, s, _ = gates3.shape
    nc = s // L
    G = min(MLSTM_GATE_CHUNKS, nc)
    return pl.pallas_call(
        functools.partial(_mlstm_gates_kernel, heads=heads, dh=dh, L=L),
        out_shape=(jax.ShapeDtypeStruct((bsz, s, LANES), F32),
                   jax.ShapeDtypeStruct((bsz, nc, SUBLANES, L), F32),
                   jax.ShapeDtypeStruct((bsz, nc, SUBLANES, LANES), F32)),
        grid=(bsz, nc // G),
        in_specs=[pl.BlockSpec((1, G * L, LANES), lambda b, c: (b, c, 0))],
        out_specs=(pl.BlockSpec((1, G * L, LANES), lambda b, c: (b, c, 0)),
                   pl.BlockSpec((1, G, SUBLANES, L), lambda b, c: (b, c, 0, 0)),
                   pl.BlockSpec((1, G, SUBLANES, LANES), lambda b, c: (b, c, 0, 0))),
        scratch_shapes=[pltpu.VMEM((SUBLANES, LANES), F32)],
        compiler_params=pltpu.CompilerParams(
            dimension_semantics=("parallel", "arbitrary"), vmem_limit_bytes=VMEM_LIMIT),
        name="mlstm_gates",
    )(gates3)


def _mlstm_kernel(q_ref, k_ref, v_ref, o_ref, cols_ref, gs_ref, dec_ref, ng_ref, out_ref,
                  c_ref, *, heads, dh):
    L = q_ref.shape[1]

    @pl.when(pl.program_id(1) == 0)
    def _():
        c_ref[...] = jnp.zeros_like(c_ref)

    cols = cols_ref[0]
    gs = gs_ref[0, 0]
    decay = dec_ref[0, 0]
    row = lax.broadcasted_iota(jnp.int32, (L, L), 0)
    col = lax.broadcasted_iota(jnp.int32, (L, L), 1)
    lower = row >= col
    ones = jnp.ones((L, LANES), BF16)
    col_of = lambda j, h: cols[:, j * SUBLANES + h:j * SUBLANES + h + 1]

    H = range(heads)
    sls = [slice(h * dh, (h + 1) * dh) for h in H]
    qs = [q_ref[0, :, sls[h]] for h in H]
    ks = [k_ref[0, :, sls[h]] for h in H]
    vas = [jnp.concatenate([v_ref[0, :, sls[h]], ones], axis=1) for h in H]
    ws = [jnp.exp(jnp.where(lower, gs[h:h + 1, :] - col_of(0, h), -jnp.inf)) for h in H]
    qk = [lax.dot_general(qs[h], ks[h], (((1,), (1,)), ((), ())), preferred_element_type=F32)
          for h in H]
    ss = [(qk[h] * ws[h]).astype(BF16) for h in H]
    cps = [c_ref[h] for h in H]
    inter = [_dot(qs[h], cps[h].astype(BF16)) for h in H]
    intra = [_dot(ss[h], vas[h]) for h in H]
    nas = [col_of(1, h) * inter[h] + intra[h] for h in H]
    invs = [1.0 / jnp.maximum(jnp.abs(nas[h][:, dh:dh + LANES]), col_of(2, h)) for h in H]
    hhs = [nas[h][:, :dh] * jnp.concatenate([invs[h]] * (dh // LANES), axis=1) for h in H]
    kws = [(ks[h].astype(F32) * col_of(3, h)).astype(BF16) for h in H]
    upd = [lax.dot_general(kws[h], vas[h], (((0,), (0,)), ((), ())), preferred_element_type=F32)
           for h in H]
    for h in H:
        c_ref[h] = decay[h:h + 1, 0:1] * cps[h] + upd[h]
    mus = [jnp.mean(hhs[h], axis=-1, keepdims=True) for h in H]
    xcs = [hhs[h] - mus[h] for h in H]
    vrs = [jnp.mean(xcs[h] * xcs[h], axis=-1, keepdims=True) for h in H]
    hns = [xcs[h] * lax.rsqrt(vrs[h] + LN_EPS) * ng_ref[h:h + 1, :] for h in H]
    for h in H:
        og = o_ref[0, :, sls[h]].astype(F32)
        out_ref[0, :, sls[h]] = (hns[h] * _sigmoid(og)).astype(BF16)


def _mlstm(p3, gates3, norm_g, *, heads, dh):
    bsz, s, _ = p3.shape
    dm = heads * dh
    L = MLSTM_L
    cols, gs, dec = _mlstm_gates(gates3, heads=heads, dh=dh, L=L)
    blk = lambda off: pl.BlockSpec((1, L, dm), lambda b, c, off=off: (b, c, off))
    return pl.pallas_call(
        functools.partial(_mlstm_kernel, heads=heads, dh=dh),
        out_shape=jax.ShapeDtypeStruct((bsz, s, dm), BF16),
        grid=(bsz, s // L),
        in_specs=[blk(0), blk(1), blk(2), blk(3),
                  pl.BlockSpec((1, L, LANES), lambda b, c: (b, c, 0)),
                  pl.BlockSpec((1, 1, SUBLANES, L), lambda b, c: (b, c, 0, 0)),
                  pl.BlockSpec((1, 1, SUBLANES, LANES), lambda b, c: (b, c, 0, 0)),
                  pl.BlockSpec((heads, dh), lambda b, c: (0, 0))],
        out_specs=pl.BlockSpec((1, L, dm), lambda b, c: (b, c, 0)),
        scratch_shapes=[pltpu.VMEM((heads, dh, dh + LANES), F32)],
        compiler_params=pltpu.CompilerParams(
            dimension_semantics=("parallel", "arbitrary"), vmem_limit_bytes=VMEM_LIMIT),
        name="mlstm",
    )(p3, p3, p3, p3, cols, gs, dec, norm_g)


def _rglru_kernel(x_ref, g_ref, cw_ref, cb_ref, wa_ref, ba_ref, wx_ref, bx_ref, lam_ref,
                  out_ref, xc_ref, a_ref, u_ref, hist_ref, hc_ref, *, nbatch, nblk):
    T = x_ref.shape[1]
    H = CONV_WIDTH - 1

    @pl.when(pl.program_id(0) == 0)
    def _():
        hist_ref[...] = jnp.zeros_like(hist_ref)
        hc_ref[...] = jnp.zeros_like(hc_ref)

    cb = cb_ref[...]
    sp8 = (-RG_C) * _softplus(-lam_ref[...])
    for b in range(nbatch):
        acc = cb + cw_ref[H] * x_ref[b, pl.ds(H, T - H)]
        for tap in range(H):
            acc = acc + cw_ref[tap] * x_ref[b, pl.ds(tap, T - H)]
        xc_ref[b, pl.ds(H * SUBLANES, (T - H) * SUBLANES), :] = acc.reshape((T - H) * SUBLANES, LANES)
        for t in range(H):
            a0 = cb
            for tap in range(CONV_WIDTH):
                src = t - H + tap
                xv = x_ref[b, src] if src >= 0 else hist_ref[b, H + src]
                a0 = a0 + cw_ref[tap] * xv
            xc_ref[b, pl.ds(t * SUBLANES, SUBLANES), :] = a0
        for t in range(H):
            hist_ref[b, t] = x_ref[b, T - H + t]

        for nb in range(nblk):
            sl = slice(nb * RG_BLOCK_DIM, (nb + 1) * RG_BLOCK_DIM)
            xb = xc_ref[b, pl.ds(nb, T, stride=SUBLANES), :]
            xb16 = xb.astype(BF16)
            r = _sigmoid(_dot(xb16, wa_ref[nb]) + ba_ref[:, sl])
            i = _sigmoid(_dot(xb16, wx_ref[nb]) + bx_ref[:, sl])
            log_a = r * sp8[:, sl]
            a = jnp.exp(log_a)
            a_ref[b, pl.ds(nb, T, stride=SUBLANES), :] = a
            u_ref[b, pl.ds(nb, T, stride=SUBLANES), :] = (
                _sqrt_nonneg(-jnp.tanh(log_a) * (a * a + 1.0)) * (i * xb))

    def body(t, hs):
        r0 = pl.multiple_of(t * SUBLANES, SUBLANES)
        out = []
        for b in range(nbatch):
            hn = a_ref[b, pl.ds(r0, SUBLANES), :] * hs[b] + u_ref[b, pl.ds(r0, SUBLANES), :]
            u_ref[b, pl.ds(r0, SUBLANES), :] = hn
            out.append(hn)
        return tuple(out)

    hs = lax.fori_loop(0, T, body, tuple(hc_ref[b] for b in range(nbatch)), unroll=8)
    for b in range(nbatch):
        hc_ref[b] = hs[b]

    for b in range(nbatch):
        for nb in range(nblk):
            sl = slice(nb * RG_BLOCK_DIM, (nb + 1) * RG_BLOCK_DIM)
            hn = u_ref[b, pl.ds(nb, T, stride=SUBLANES), :]
            out_ref[b, :, sl] = (hn * _gelu_tanh(g_ref[b, :, sl].astype(F32))).astype(BF16)


def _rglru(x_tm, p3, conv_w, conv_b, wa, ba, wx, bx, lam, *, g_blk):
    bsz, s, _, _ = x_tm.shape
    dr = conv_w.shape[1]
    assert dr == SUBLANES * LANES
    nblk = dr // RG_BLOCK_DIM
    T = RG_T
    vec = lambda: pl.BlockSpec((1, dr), lambda t: (0, 0))
    return pl.pallas_call(
        functools.partial(_rglru_kernel, nbatch=bsz, nblk=nblk),
        out_shape=jax.ShapeDtypeStruct((bsz, s, dr), BF16),
        grid=(s // T,),
        in_specs=[
            pl.BlockSpec((bsz, T, SUBLANES, LANES), lambda t: (0, t, 0, 0)),
            pl.BlockSpec((bsz, T, dr), lambda t: (0, t, g_blk)),
            pl.BlockSpec((CONV_WIDTH, SUBLANES, LANES), lambda t: (0, 0, 0)),
            pl.BlockSpec((SUBLANES, LANES), lambda t: (0, 0)),
            pl.BlockSpec((nblk, RG_BLOCK_DIM, RG_BLOCK_DIM), lambda t: (0, 0, 0)),
            vec(),
            pl.BlockSpec((nblk, RG_BLOCK_DIM, RG_BLOCK_DIM), lambda t: (0, 0, 0)),
            vec(),
            vec(),
        ],
        out_specs=pl.BlockSpec((bsz, T, dr), lambda t: (0, t, 0)),
        scratch_shapes=[pltpu.VMEM((bsz, T * SUBLANES, LANES), F32),
                        pltpu.VMEM((bsz, T * SUBLANES, LANES), F32),
                        pltpu.VMEM((bsz, T * SUBLANES, LANES), F32),
                        pltpu.VMEM((bsz, CONV_WIDTH - 1, SUBLANES, LANES), F32),
                        pltpu.VMEM((bsz, SUBLANES, LANES), F32)],
        compiler_params=pltpu.CompilerParams(
            dimension_semantics=("arbitrary",), vmem_limit_bytes=VMEM_LIMIT),
        name="rglru",
    )(x_tm, p3, conv_w.reshape(CONV_WIDTH, SUBLANES, LANES), conv_b.reshape(SUBLANES, LANES),
      wa, ba, wx, bx, lam)


def _outproj_kernel(x_ref, hm_ref, hr_ref, mod_ref, wt_ref, wb_ref, lng_ref, lnb_ref, o_ref):
    y = _dot(hm_ref[...], wt_ref[...]) + _dot(hr_ref[...], wb_ref[...])
    gate = mod_ref[0, 5:6, :]
    r = ALPHA * x_ref[...] + gate * y
    o_ref[...] = _ln(r) * lng_ref[...] + lnb_ref[...]


def _outproj(x2d, hm, hr, mod, w_out, lng, lnb, *, rows_per_batch):
    m, d = x2d.shape
    dm = hm.shape[1]
    dr = hr.shape[1]
    assert dm == dr and w_out.shape == (dm + dr, d)
    tpb = rows_per_batch // OUT_TM
    return pl.pallas_call(
        _outproj_kernel,
        out_shape=jax.ShapeDtypeStruct((m, d), F32),
        grid=(m // OUT_TM,),
        in_specs=[
            pl.BlockSpec((OUT_TM, d), lambda i: (i, 0)),
            pl.BlockSpec((OUT_TM, dm), lambda i: (i, 0)),
            pl.BlockSpec((OUT_TM, dr), lambda i: (i, 0)),
            pl.BlockSpec((1, 3 * N_SUB, d), lambda i: (i // tpb, 0, 0)),
            pl.BlockSpec((dm, d), lambda i: (0, 0)),
            pl.BlockSpec((dr, d), lambda i: (1, 0)),
            pl.BlockSpec((1, d), lambda i: (0, 0)),
            pl.BlockSpec((1, d), lambda i: (0, 0)),
        ],
        out_specs=pl.BlockSpec((OUT_TM, d), lambda i: (i, 0)),
        compiler_params=pltpu.CompilerParams(
            dimension_semantics=("parallel",), vmem_limit_bytes=VMEM_LIMIT),
        name="outproj",
    )(x2d, hm, hr, mod, w_out, w_out, lng, lnb)


def kernel(x, c, w_ada, b_ada, ffn1_w13, ffn1_w2, w_in, b_in, mlstm_norm_g, rg_conv_w,
           rg_conv_b, rg_wa, rg_ba, rg_wx, rg_bx, rg_lambda, w_out, ffn2_w13, ffn2_w2,
           ln_g, ln_b):
    bsz, s, d = x.shape
    depth = w_ada.shape[0]
    heads = mlstm_norm_g.shape[1]
    dh = mlstm_norm_g.shape[2]
    dm = heads * dh
    dr = rg_conv_w.shape[2]
    m = bsz * s
    x2d = x.reshape(m, d)
    for l in range(depth):
        mod = _adaln(c, w_ada[l], b_ada[l]).reshape(bsz, 3 * N_SUB, d)

        x2d = _ffn(x2d, mod, ffn1_w13[l].astype(BF16), ffn1_w2[l].astype(BF16),
                   ln_g[l, 0:1], ln_b[l, 0:1], sub=0, rows_per_batch=s)

        wl, bl = w_in[l], b_in[l]
        g0 = 4 * dm
        r0 = g0 + 2 * heads
        w_gate = jnp.pad(wl[:, g0:r0], ((0, 0), (0, LANES - 2 * heads))).astype(BF16)
        b_gate = jnp.pad(bl[g0:r0], (0, LANES - 2 * heads)).reshape(1, LANES)
        p, gates, x_tm = _inproj(x2d, mod, wl[:, :g0].astype(BF16), bl[:g0].reshape(1, -1),
                                 wl[:, r0:].astype(BF16), bl[r0:].reshape(1, -1),
                                 w_gate, b_gate, rows_per_batch=s)
        p3 = p.reshape(bsz, s, -1)
        gates3 = gates.reshape(bsz, s, LANES)
        x_tm = x_tm.reshape(bsz, s, SUBLANES, LANES)

        h_m = _mlstm(p3, gates3, mlstm_norm_g[l], heads=heads, dh=dh)
        h_r = _rglru(x_tm, p3, rg_conv_w[l], rg_conv_b[l],
                     rg_wa[l].astype(BF16), rg_ba[l].reshape(1, dr),
                     rg_wx[l].astype(BF16), rg_bx[l].reshape(1, dr),
                     rg_lambda[l].reshape(1, dr), g_blk=g0 // dr)

        x2d = _outproj(x2d, h_m.reshape(m, dm), h_r.reshape(m, dr), mod,
                       w_out[l].astype(BF16), ln_g[l, 1:2], ln_b[l, 1:2], rows_per_batch=s)

        x2d = _ffn(x2d, mod, ffn2_w13[l].astype(BF16), ffn2_w2[l].astype(BF16),
                   ln_g[l, 2:3], ln_b[l, 2:3], sub=2, rows_per_batch=s)
    return x2d.reshape(bsz, s, d)
```
